```python
import jax, jax.numpy as jnp
from jax import lax
import numpy as np

D_MODEL = 2048
BATCH = 4
SEQ = 2048
DEPTH = 1
DEC_BATCH = 128
DEC_SEQ = 1
PAST_LEN = 2048
PAGE_SIZE = 128

M_HEADS = 4
M_DK = 256
M_DV = 256
M_WIDTH = M_HEADS * M_DV
M_CHUNK = 128
CONV_W = 4
B_HEADS = 8
B_DH = 128
B_WIDTH = B_HEADS * B_DH
MOBA_BLOCK = 256
MOBA_TOPK = 3
Q_BLOCK = 16
P_HEADS = 8
P_NKEYS = 128
P_EXPERTS = P_NKEYS * P_NKEYS
P_DK = 256
P_TOPK = 16
P_TOKEN_BLOCK = 256
EPS = 1e-6

C_QK_M = 2 * M_HEADS * M_DK
C_V_M = M_WIDTH
C_O_M = M_WIDTH
C_IF_M = 2 * M_HEADS
C_GATE = 2 * D_MODEL
D_IN = C_QK_M + C_V_M + C_O_M + C_IF_M + 3 * B_WIDTH + C_GATE

kernel_name = 'hybrid_mlstm_moba_peer_step'


def _rmsnorm(x, w):
    xf = x.astype(jnp.float32)
    y = xf * lax.rsqrt(jnp.mean(xf * xf, axis=-1, keepdims=True) + EPS)
    return (y * w.astype(jnp.float32)).astype(x.dtype)


def _block_len(t, pref):
    return pref if t % pref == 0 else t


def _alibi_slopes():
    return 2.0 ** (-8.0 * jnp.arange(1, B_HEADS + 1, dtype=jnp.float32) / B_HEADS)


def _causal_conv(xc, w, b):
    t = xc.shape[1] - (CONV_W - 1)
    out = b
    for j in range(CONV_W):
        out = out + xc[:, j:j + t] * w[j]
    return out


def _mlstm(q, k, v, li, lf, c0, n0, m0):
    bsz, t, nh, _ = q.shape
    L = _block_len(t, M_CHUNK)
    nc = t // L

    def to_chunks(a):
        a = a.astype(jnp.float32).reshape((bsz, nc, L) + a.shape[2:])
        return jnp.moveaxis(jnp.moveaxis(a, 1, 0), 3, 2)

    causal = jnp.tril(jnp.ones((L, L), dtype=bool))

    def step(carry, xs):
        c, n, m = carry
        qc, kc, vc, lic, lfc = xs
        b = jnp.cumsum(lfc, axis=-1)
        d = jnp.where(causal, b[..., :, None] - b[..., None, :] + lic[..., None, :], -jnp.inf)
        inter = b + m[..., None]
        m_t = jnp.maximum(inter, jnp.max(d, axis=-1))
        w_intra = jnp.exp(d - m_t[..., None])
        w_inter = jnp.exp(inter - m_t)
        s = jnp.einsum('bhtd,bhsd->bhts', qc, kc) * w_intra
        num = w_inter[..., None] * jnp.einsum('bhtd,bhde->bhte', qc, c) + jnp.einsum('bhts,bhse->bhte', s, vc)
        qn = w_inter * jnp.einsum('bhtd,bhd->bht', qc, n) + jnp.sum(s, axis=-1)
        h = num / jnp.maximum(jnp.abs(qn), jnp.exp(-m_t))[..., None]
        m_new = m_t[..., -1]
        w_end = jnp.exp(b[..., -1:] - b + lic - m_new[..., None])
        decay = jnp.exp(b[..., -1] + m - m_new)
        c = decay[..., None, None] * c + jnp.einsum('bhs,bhsd,bhse->bhde', w_end, kc, vc)
        n = decay[..., None] * n + jnp.einsum('bhs,bhsd->bhd', w_end, kc)
        return (c, n, m_new), h

    init = (c0.astype(jnp.float32), n0.astype(jnp.float32), m0.astype(jnp.float32))
    (c1, n1, m1), h = lax.scan(step, init, tuple(to_chunks(a) for a in (q, k, v, li, lf)))
    h = jnp.moveaxis(jnp.moveaxis(h, 2, 3), 0, 1).reshape(bsz, t, nh, M_DV)
    return h, c1, n1, m1


def _moba(q, k_all, v_all, slopes):
    bsz, t, nh, dh = q.shape
    lk = k_all.shape[1]
    pos0 = lk - t
    nb = -(-lk // MOBA_BLOCK)
    pad = nb * MOBA_BLOCK - lk
    kb = jnp.pad(k_all, ((0, 0), (0, pad), (0, 0), (0, 0))).reshape(bsz, nb, MOBA_BLOCK, nh, dh)
    vb = jnp.pad(v_all, ((0, 0), (0, pad), (0, 0), (0, 0))).reshape(bsz, nb, MOBA_BLOCK, nh, dh)
    kmean = jnp.mean(kb, axis=2, dtype=jnp.float32)
    qb = _block_len(t, Q_BLOCK)
    nq = t // qb
    q_blocks = jnp.moveaxis(q.reshape(bsz, nq, qb, nh, dh), 1, 0)
    pos_blocks = (pos0 + jnp.arange(t, dtype=jnp.int32)).reshape(nq, qb)
    b_idx = jnp.arange(bsz)[:, None, None, None]
    h_idx = jnp.arange(nh)[None, :, None, None]
    offs = jnp.arange(MOBA_BLOCK, dtype=jnp.int32)
    n_cand = max(nb, MOBA_TOPK)
    scale = dh ** -0.5

    def one(args):
        qc, pos = args
        bq = pos // MOBA_BLOCK
        gate = jnp.einsum('bqhd,bnhd->bhqn', qc.astype(jnp.float32), kmean)
        gate = jnp.where(jnp.arange(nb)[None, :] < bq[:, None], gate, -jnp.inf)
        if n_cand > nb:
            gate = jnp.pad(gate, ((0, 0), (0, 0), (0, 0), (0, n_cand - nb)), constant_values=-jnp.inf)
        _, sel = lax.top_k(gate, MOBA_TOPK)
        sel = jnp.minimum(sel, nb - 1)
        own = jnp.broadcast_to(bq[None, None, :, None], (bsz, nh, qb, 1)).astype(sel.dtype)
        blocks = jnp.concatenate([sel, own], axis=-1)
        slot_ok = jnp.concatenate([jnp.arange(MOBA_TOPK)[None, :] < bq[:, None],
                                   jnp.ones((qb, 1), dtype=bool)], axis=-1)
        kg = kb[b_idx, blocks, :, h_idx, :]
        vg = vb[b_idx, blocks, :, h_idx, :]
        kpos = blocks[..., None] * MOBA_BLOCK + offs
        dist = (pos[None, None, :, None, None] - kpos).astype(jnp.float32)
        s = jnp.einsum('bqhd,bhqsrd->bhqsr', qc, kg).astype(jnp.float32) * scale \
            - slopes[None, :, None, None, None] * dist
        ok = slot_ok[None, None, :, :, None] & (dist >= 0)
        s = jnp.where(ok, s, -jnp.inf)
        p = jax.nn.softmax(s.reshape(bsz, nh, qb, -1), axis=-1).reshape(s.shape)
        return jnp.einsum('bhqsr,bhqsrd->bqhd', p.astype(vg.dtype), vg)

    out = lax.map(one, (q_blocks, pos_blocks))
    return jnp.moveaxis(out, 0, 1).reshape(bsz, t, nh, dh)


def _peer(h, wq, keys1, keys2, u_tab, v_tab):
    bsz, t, d = h.shape
    n = bsz * t
    blk = min(P_TOKEN_BLOCK, n)
    nb = -(-n // blk)
    tok = jnp.pad(h.reshape(n, d), ((0, nb * blk - n), (0, 0))).reshape(nb, blk, d)
    half = P_DK // 2

    def one(tb):
        q = (tb @ wq).astype(jnp.float32).reshape(blk, P_HEADS, P_DK)
        s1 = jnp.einsum('thd,hnd->thn', q[..., :half], keys1.astype(jnp.float32))
        s2 = jnp.einsum('thd,hnd->thn', q[..., half:], keys2.astype(jnp.float32))
        v1, i1 = lax.top_k(s1, P_TOPK)
        v2, i2 = lax.top_k(s2, P_TOPK)
        cand = (v1[..., :, None] + v2[..., None, :]).reshape(blk, P_HEADS, P_TOPK * P_TOPK)
        vals, ci = lax.top_k(cand, P_TOPK)
        e = jnp.take_along_axis(i1, ci // P_TOPK, axis=-1) * P_NKEYS + jnp.take_along_axis(i2, ci % P_TOPK, axis=-1)
        g = jax.nn.softmax(vals, axis=-1)
        act = jax.nn.gelu(jnp.einsum('thkd,td->thk', u_tab[e], tb).astype(jnp.float32), approximate=False)
        return jnp.einsum('thk,thkd->td', (g * act).astype(tb.dtype), v_tab[e])

    out = lax.map(one, tok).reshape(nb * blk, d)[:n]
    return out.reshape(bsz, t, d)


def _layer(x, conv_buf, c0, n0, m0, k_past, v_past, p):
    bsz, t, _ = x.shape
    h = _rmsnorm(x, p['norm1_w'])
    proj = h @ p['w_in']
    cuts = np.cumsum([C_QK_M, C_V_M, C_O_M, C_IF_M, B_WIDTH, B_WIDTH, B_WIDTH]).tolist()
    qk_m, v_m, o_m, if_m, q_b, k_b, v_b, gates = jnp.split(proj, cuts, axis=-1)
    conv_in = jnp.concatenate([conv_buf.astype(qk_m.dtype), qk_m], axis=1)
    new_conv = conv_in[:, -(CONV_W - 1):]
    qk_c = jax.nn.silu(_causal_conv(conv_in, p['conv_w'], p['conv_b']))
    q_m, k_m = jnp.split(qk_c, 2, axis=-1)
    q_m = q_m.reshape(bsz, t, M_HEADS, M_DK)
    k_m = k_m.reshape(bsz, t, M_HEADS, M_DK) * (M_DK ** -0.5)
    v_m = v_m.reshape(bsz, t, M_HEADS, M_DV)
    if_f = if_m.astype(jnp.float32) + p['b_gates'].astype(jnp.float32)
    li = if_f[..., :M_HEADS]
    lf = jax.nn.log_sigmoid(if_f[..., M_HEADS:])
    h_m, c1, n1, m1 = _mlstm(q_m, k_m, v_m, li, lf, c0, n0, m0)
    h_m = _rmsnorm(h_m, p['mlstm_norm_w']).reshape(bsz, t, M_WIDTH).astype(x.dtype)
    u_a = (jax.nn.sigmoid(o_m) * h_m) @ p['w_branch_a']
    q_b = _rmsnorm(q_b.reshape(bsz, t, B_HEADS, B_DH), p['qnorm_w'])
    k_b = _rmsnorm(k_b.reshape(bsz, t, B_HEADS, B_DH), p['knorm_w'])
    v_b = v_b.reshape(bsz, t, B_HEADS, B_DH)
    k_all = jnp.concatenate([k_past.astype(k_b.dtype), k_b], axis=1)
    v_all = jnp.concatenate([v_past.astype(v_b.dtype), v_b], axis=1)
    o_b = _moba(q_b, k_all, v_all, _alibi_slopes())
    u_b = o_b.reshape(bsz, t, B_WIDTH) @ p['w_branch_b']
    g_a, g_b = jnp.split(jax.nn.sigmoid(gates), 2, axis=-1)
    x = x + (g_a * u_a + g_b * u_b) @ p['w_out']
    x = x + _peer(_rmsnorm(x, p['norm2_w']), p['peer_wq'], p['peer_keys1'], p['peer_keys2'], p['peer_u'], p['peer_v'])
    return x, (k_b, v_b, c1, n1, m1, new_conv)


def setup_inputs(seed: int = 0) -> dict:
    key = jax.random.key(seed)
    keys = jax.random.split(key, 32)
    f32 = jnp.float32

    def nrm(i, shape, scale):
        return jax.random.normal(keys[i], shape, f32) * scale

    n_pages = PAST_LEN // PAGE_SIZE
    n_used = DEC_BATCH * n_pages
    n_pool = n_used + max(1, n_used // 4)
    L = DEPTH
    page_table = jax.random.permutation(keys[9], n_pool)[:n_used].reshape(DEC_BATCH, n_pages).astype(jnp.int32)
    b_gates = jnp.concatenate([nrm(14, (L, M_HEADS), 0.1),
                               jnp.linspace(3.0, 6.0, M_HEADS, dtype=f32)[None, :] + nrm(15, (L, M_HEADS), 0.1)], axis=-1)
    return {
        'x_prompt': nrm(0, (BATCH, SEQ, D_MODEL), 1.0),
        'x_sample': nrm(1, (DEC_BATCH, DEC_SEQ, D_MODEL), 1.0),
        'cache_k': nrm(2, (L, n_pool, PAGE_SIZE, B_HEADS, B_DH), 1.0),
        'cache_v': nrm(3, (L, n_pool, PAGE_SIZE, B_HEADS, B_DH), 1.0),
        'state_C': nrm(4, (L, DEC_BATCH, M_HEADS, M_DK, M_DV), M_DK ** -0.5),
        'state_n': nrm(5, (L, DEC_BATCH, M_HEADS, M_DK), M_DK ** -0.5),
        'state_m': nrm(6, (L, DEC_BATCH, M_HEADS), 0.5),
        'state_conv': nrm(7, (L, DEC_BATCH, CONV_W - 1, C_QK_M), 1.0),
        'page_table': page_table,
        'norm1_w': 1.0 + nrm(12, (L, D_MODEL), 0.02),
        'w_in': nrm(13, (L, D_MODEL, D_IN), D_MODEL ** -0.5),
        'b_gates': b_gates,
        'conv_w': nrm(16, (L, CONV_W, C_QK_M), CONV_W ** -0.5),
        'conv_b': nrm(17, (L, C_QK_M), 0.02),
        'mlstm_norm_w': 1.0 + nrm(18, (L, M_HEADS, M_DV), 0.02),
        'qnorm_w': 1.0 + nrm(19, (L, B_HEADS, B_DH), 0.02),
        'knorm_w': 1.0 + nrm(20, (L, B_HEADS, B_DH), 0.02),
        'w_branch_a': nrm(21, (L, M_WIDTH, D_MODEL), M_WIDTH ** -0.5),
        'w_branch_b': nrm(22, (L, B_WIDTH, D_MODEL), B_WIDTH ** -0.5),
        'w_out': nrm(23, (L, D_MODEL, D_MODEL), D_MODEL ** -0.5),
        'norm2_w': 1.0 + nrm(24, (L, D_MODEL), 0.02),
        'peer_wq': nrm(25, (L, D_MODEL, P_HEADS * P_DK), D_MODEL ** -0.5),
        'peer_keys1': nrm(26, (L, P_HEADS, P_NKEYS, P_DK // 2), (P_DK // 2) ** -0.5),
        'peer_keys2': nrm(27, (L, P_HEADS, P_NKEYS, P_DK // 2), (P_DK // 2) ** -0.5),
        'peer_u': nrm(28, (L, P_EXPERTS, D_MODEL), D_MODEL ** -0.5),
        'peer_v': nrm(29, (L, P_EXPERTS, D_MODEL), P_HEADS ** -0.5),
    }


def reference(x_prompt, x_sample, cache_k, cache_v, state_C, state_n, state_m, state_conv, page_table,
              norm1_w, w_in, b_gates, conv_w, conv_b, mlstm_norm_w, qnorm_w, knorm_w,
              w_branch_a, w_branch_b, w_out, norm2_w, peer_wq, peer_keys1, peer_keys2, peer_u, peer_v):
    n_pages = page_table.shape[1]
    bp = x_prompt.shape[0]
    bs = x_sample.shape[0]
    x_p, x_s = x_prompt, x_sample
    states_p, states_s = [], []
    for l in range(DEPTH):
        p = dict(norm1_w=norm1_w[l], w_in=w_in[l], b_gates=b_gates[l], conv_w=conv_w[l], conv_b=conv_b[l],
                 mlstm_norm_w=mlstm_norm_w[l], qnorm_w=qnorm_w[l], knorm_w=knorm_w[l],
                 w_branch_a=w_branch_a[l], w_branch_b=w_branch_b[l], w_out=w_out[l], norm2_w=norm2_w[l],
                 peer_wq=peer_wq[l], peer_keys1=peer_keys1[l], peer_keys2=peer_keys2[l],
                 peer_u=peer_u[l], peer_v=peer_v[l])
        x_p, st_p = _layer(
            x_p,
            jnp.zeros((bp, CONV_W - 1, C_QK_M), x_p.dtype),
            jnp.zeros((bp, M_HEADS, M_DK, M_DV), jnp.float32),
            jnp.zeros((bp, M_HEADS, M_DK), jnp.float32),
            jnp.zeros((bp, M_HEADS), jnp.float32),
            jnp.zeros((bp, 0, B_HEADS, B_DH), x_p.dtype),
            jnp.zeros((bp, 0, B_HEADS, B_DH), x_p.dtype),
            p)
        k_past = cache_k[l][page_table].reshape(bs, n_pages * PAGE_SIZE, B_HEADS, B_DH)
        v_past = cache_v[l][page_table].reshape(bs, n_pages * PAGE_SIZE, B_HEADS, B_DH)
        x_s, st_s = _layer(x_s, state_conv[l], state_C[l], state_n[l], state_m[l], k_past, v_past, p)
        states_p.append(st_p)
        states_s.append(st_s)
    k_p, v_p, c_p, n_p, m_p, conv_p = [jnp.stack(a) for a in zip(*states_p)]
    k_s, v_s, c_s, n_s, m_s, conv_s = [jnp.stack(a) for a in zip(*states_s)]
    return (x_p, x_s, k_p, v_p, c_p, n_p, m_p, conv_p, k_s, v_s, c_s, n_s, m_s, conv_s)
```

```python
import functools

import jax
import jax.numpy as jnp
from jax import lax
from jax.experimental import pallas as pl
from jax.experimental.pallas import tpu as pltpu

f32 = jnp.float32
bf16 = jnp.bfloat16

M_HEADS = 4
M_DK = 256
M_DV = 256
M_WIDTH = M_HEADS * M_DV
M_CHUNK = 128
CONV_W = 4
B_HEADS = 8
B_DH = 128
B_WIDTH = B_HEADS * B_DH
MOBA_BLOCK = 256
MOBA_TOPK = 3
P_HEADS = 8
P_NKEYS = 128
P_DK = 256
P_TOPK = 16
EPS = 1e-6

LANES = 128
SUBLANES = 8
VMEM_LIMIT = 56 * 1024 * 1024

C_QK = 0
C_GATE = 2 * M_HEADS * M_DK
C_VM = C_GATE + 2 * 2048
NEG_INF = float("-inf")


def _cparams(sem):
    return pltpu.CompilerParams(dimension_semantics=sem, vmem_limit_bytes=VMEM_LIMIT)


def _rms(x, w):
    y = x * lax.rsqrt(jnp.mean(x * x, axis=-1, keepdims=True) + EPS)
    return y * w


def _log_sigmoid(x):
    return -(jnp.maximum(-x, 0.0) + jnp.log(1.0 + jnp.exp(-jnp.abs(x))))


def _split3(a):
    p0 = a.astype(bf16)
    r = a - p0.astype(f32)
    p1 = r.astype(bf16)
    r = r - p1.astype(f32)
    return p0, p1, r.astype(bf16)


def _dot(a, b):
    return jnp.dot(a, b, preferred_element_type=f32)


def _dot_nt(a, b):
    return lax.dot_general(a, b, (((1,), (1,)), ((), ())), preferred_element_type=f32)


def _dot_tn(a, b):
    return lax.dot_general(a, b, (((0,), (0,)), ((), ())), preferred_element_type=f32)


def _single(shape, index_map):
    return pl.BlockSpec(shape, index_map, pipeline_mode=pl.Buffered(1))


def _inproj_kernel(x_ref, nw_ref, w_ref, wif_ref, proj_ref, ifp_ref, h_scr):
    @pl.when(pl.program_id(1) == 0)
    def _():
        h = _rms(x_ref[...], nw_ref[...]).astype(bf16)
        h_scr[...] = h
        ifp_ref[...] = _dot(h, wif_ref[...])

    proj_ref[...] = _dot(h_scr[...], w_ref[...])


def _inproj(x, norm_w, w_main, w_if, tm, tn):
    n, d = x.shape
    ncol = w_main.shape[1]
    return pl.pallas_call(
        _inproj_kernel,
        out_shape=(jax.ShapeDtypeStruct((n, ncol), f32), jax.ShapeDtypeStruct((n, LANES), f32)),
        grid=(n // tm, ncol // tn),
        in_specs=[
            pl.BlockSpec((tm, d), lambda i, j: (i, 0)),
            pl.BlockSpec((1, d), lambda i, j: (0, 0)),
            pl.BlockSpec((d, tn), lambda i, j: (0, j)),
            pl.BlockSpec((d, LANES), lambda i, j: (0, 0)),
        ],
        out_specs=(pl.BlockSpec((tm, tn), lambda i, j: (i, j)), pl.BlockSpec((tm, LANES), lambda i, j: (i, 0))),
        scratch_shapes=[pltpu.VMEM((tm, d), bf16)],
        compiler_params=_cparams(("parallel", "arbitrary")),
        name="inproj",
    )(x, norm_w, w_main, w_if)


def _mlstm_prompt_kernel(qk_ref, v_ref, o_ref, ifc_ref, ifr_ref, cw_ref, cb_ref, bgr_ref, bgc_ref, nw_ref,
                         hg_ref, c_ref, n_ref, m_ref, prev_scr):
    L = M_CHUNK

    @pl.when(pl.program_id(1) == 0)
    def _():
        prev_scr[...] = jnp.zeros_like(prev_scr)
        c_ref[...] = jnp.zeros_like(c_ref)
        n_ref[...] = jnp.zeros_like(n_ref)
        m_ref[...] = jnp.zeros_like(m_ref)

    x = qk_ref[...]
    prev = prev_scr[...]
    rows = lax.broadcasted_iota(jnp.int32, x.shape, 0)
    acc = cb_ref[...]
    for j in range(CONV_W - 1):
        sh = CONV_W - 1 - j
        xs = jnp.where(rows < sh, pltpu.roll(prev, sh, 0), pltpu.roll(x, sh, 0))
        acc = acc + xs * cw_ref[j:j + 1, :]
    acc = acc + x * cw_ref[CONV_W - 1:CONV_W, :]
    prev_scr[...] = x
    qk = acc * jax.nn.sigmoid(acc)

    r = lax.broadcasted_iota(jnp.int32, (L, L), 0)
    cidx = lax.broadcasted_iota(jnp.int32, (L, L), 1)
    causal = cidx <= r
    tri = jnp.where(causal, 1.0, 0.0).astype(bf16)
    tri_t = jnp.where(r <= cidx, 1.0, 0.0).astype(bf16)

    ifc = ifc_ref[...] + bgr_ref[...]
    ifr = ifr_ref[...] + bgc_ref[...]
    lfc = _log_sigmoid(ifc)
    lfr = _log_sigmoid(ifr)
    c0, c1, c2 = _split3(lfc)
    bcols = _dot(tri, c0) + _dot(tri, c1) + _dot(tri, c2)
    r0, r1, r2 = _split3(lfr)
    brows = _dot(r0, tri_t) + _dot(r1, tri_t) + _dot(r2, tri_t)

    for h in range(M_HEADS):
        q = qk[:, h * M_DK:(h + 1) * M_DK]
        k = qk[:, (M_HEADS + h) * M_DK:(M_HEADS + h + 1) * M_DK] * (M_DK ** -0.5)
        v = v_ref[:, h * M_DV:(h + 1) * M_DV]
        li_c = ifc[:, h:h + 1]
        li_r = ifr[h:h + 1, :]
        b_c = bcols[:, M_HEADS + h:M_HEADS + h + 1]
        b_r = brows[M_HEADS + h:M_HEADS + h + 1, :]
        m_prev = m_ref[0, h:h + 1, 0:1]
        c_st = c_ref[0, h]
        n_st = n_ref[0, h:h + 1, :]

        d = jnp.where(causal, b_c - b_r + li_r, NEG_INF)
        inter = b_c + m_prev
        m_t = jnp.maximum(inter, jnp.max(d, axis=-1, keepdims=True))
        w_intra = jnp.exp(d - m_t)
        w_inter = jnp.exp(inter - m_t)
        qb = q.astype(bf16)
        kb = k.astype(bf16)
        vb = v.astype(bf16)
        s = _dot_nt(qb, kb) * w_intra
        num = w_inter * _dot(qb, c_st.astype(bf16)) + _dot(s.astype(bf16), vb)
        qn = w_inter * jnp.sum(q * n_st, axis=-1, keepdims=True) + jnp.sum(s, axis=-1, keepdims=True)
        hh = num / jnp.maximum(jnp.abs(qn), jnp.exp(-m_t))
        m_new = m_t[L - 1:L, :]
        b_last = b_c[L - 1:L, :]
        w_end = jnp.exp(b_last - b_c + li_c - m_new)
        decay = jnp.exp(b_last + m_prev - m_new)
        kw = k * w_end
        c_ref[0, h] = decay * c_st + _dot_tn(kw.astype(bf16), vb)
        n_ref[0, h:h + 1, :] = decay * n_st + jnp.sum(kw, axis=0, keepdims=True)
        m_ref[0, h:h + 1, :] = jnp.broadcast_to(m_new, (1, LANES))

        hn = _rms(hh, nw_ref[:, h * M_DV:(h + 1) * M_DV])
        og = jax.nn.sigmoid(o_ref[:, h * M_DV:(h + 1) * M_DV])
        hg_ref[:, h * M_DV:(h + 1) * M_DV] = (og * hn).astype(hg_ref.dtype)


def _mlstm_prompt(proj, ifp, ifp_t, conv_w, conv_b, bg_row, bg_col, norm_w, bsz, t):
    L = M_CHUNK
    nc = t // L
    n = bsz * t
    qkw = 2 * M_HEADS * M_DK
    return pl.pallas_call(
        _mlstm_prompt_kernel,
        out_shape=(
            jax.ShapeDtypeStruct((n, M_WIDTH), bf16),
            jax.ShapeDtypeStruct((bsz, M_HEADS, M_DK, M_DV), f32),
            jax.ShapeDtypeStruct((bsz, M_HEADS, M_DK), f32),
            jax.ShapeDtypeStruct((bsz, 2 * M_HEADS, LANES), f32),
        ),
        grid=(bsz, nc),
        in_specs=[
            pl.BlockSpec((L, qkw), lambda b, c: (b * nc + c, 0)),
            pl.BlockSpec((L, M_WIDTH), lambda b, c: (b * nc + c, C_VM // M_WIDTH)),
            pl.BlockSpec((L, M_WIDTH), lambda b, c: (b * nc + c, C_VM // M_WIDTH + 1)),
            pl.BlockSpec((L, LANES), lambda b, c: (b * nc + c, 0)),
            pl.BlockSpec((2 * M_HEADS, L), lambda b, c: (0, b * nc + c)),
            pl.BlockSpec((CONV_W, qkw), lambda b, c: (0, 0)),
            pl.BlockSpec((1, qkw), lambda b, c: (0, 0)),
            pl.BlockSpec((1, LANES), lambda b, c: (0, 0)),
            pl.BlockSpec((2 * M_HEADS, 1), lambda b, c: (0, 0)),
            pl.BlockSpec((1, M_WIDTH), lambda b, c: (0, 0)),
        ],
        out_specs=(
            pl.BlockSpec((L, M_WIDTH), lambda b, c: (b * nc + c, 0)),
            pl.BlockSpec((1, M_HEADS, M_DK, M_DV), lambda b, c: (b, 0, 0, 0)),
            pl.BlockSpec((1, M_HEADS, M_DK), lambda b, c: (b, 0, 0)),
            pl.BlockSpec((1, 2 * M_HEADS, LANES), lambda b, c: (b, 0, 0)),
        ),
        scratch_shapes=[pltpu.VMEM((L, qkw), f32)],
        compiler_params=_cparams(("parallel", "arbitrary")),
        name="mlstm_prompt",
    )(proj, proj, proj, ifp, ifp_t, conv_w, conv_b, bg_row, bg_col, norm_w)


def _mlstm_sample_kernel(xq_ref, xk_ref, v_ref, o_ref, ifp_ref, scq_ref, sck_ref, cwq_ref, cwk_ref, cbq_ref, cbk_ref,
                         bgr_ref, nw_ref, c0_ref, n0_ref, m0_ref,
                         hg_ref, c1_ref, n1_ref, m1_ref, num_scr):
    h = pl.program_id(1)
    g_rows = xq_ref.shape[0]

    def conv(x_ref, sc_ref, cw_ref, cb_ref):
        acc = cb_ref[...]
        for j in range(CONV_W - 1):
            acc = acc + sc_ref[j] * cw_ref[j:j + 1, :]
        acc = acc + x_ref[...] * cw_ref[CONV_W - 1:CONV_W, :]
        return acc * jax.nn.sigmoid(acc)

    q = conv(xq_ref, scq_ref, cwq_ref, cbq_ref)
    k = conv(xk_ref, sck_ref, cwk_ref, cbk_ref) * (M_DK ** -0.5)
    v = v_ref[...]

    ifp = ifp_ref[...] + bgr_ref[...]
    lane = lax.broadcasted_iota(jnp.int32, ifp.shape, 1)
    li = jnp.sum(jnp.where(lane == h, ifp, 0.0), axis=-1, keepdims=True)
    lf = _log_sigmoid(jnp.sum(jnp.where(lane == h + M_HEADS, ifp, 0.0), axis=-1, keepdims=True))
    m0 = m0_ref[...]
    lane_m = lax.broadcasted_iota(jnp.int32, m0.shape, 1)
    m_prev = jnp.sum(jnp.where(lane_m == h, m0, 0.0), axis=-1, keepdims=True)
    n0 = n0_ref[...]

    inter = lf + m_prev
    m_t = jnp.maximum(inter, li)
    w_intra = jnp.exp(li - m_t)
    w_inter = jnp.exp(inter - m_t)
    s = jnp.sum(q * k, axis=-1, keepdims=True) * w_intra
    kw = k * w_intra
    for g in range(g_rows):
        c_st = c0_ref[g, 0]
        q_col = q[g:g + 1, :].reshape(M_DK, 1)
        kw_col = kw[g:g + 1, :].reshape(M_DK, 1)
        num_scr[g:g + 1, :] = jnp.sum(q_col * c_st, axis=0, keepdims=True)
        c1_ref[g, 0] = w_inter[g:g + 1, :] * c_st + kw_col * v[g:g + 1, :]
    num = w_inter * num_scr[...] + s * v
    qn = w_inter * jnp.sum(q * n0, axis=-1, keepdims=True) + s
    hh = num / jnp.maximum(jnp.abs(qn), jnp.exp(-m_t))
    n1_ref[...] = w_inter * n0 + kw
    m1_ref[0] = jnp.broadcast_to(m_t, (g_rows, LANES))
    hg_ref[...] = jax.nn.sigmoid(o_ref[...]) * _rms(hh, nw_ref[...])


def _mlstm_sample(proj, ifp, state_conv, conv_w, conv_b, bg_row, norm_w, c0, n0, m0):
    bsz = proj.shape[0]
    g = SUBLANES
    nvm = C_VM // M_DV
    row_blk = lambda w, off: pl.BlockSpec((g, w), lambda i, h, off=off: (i, off + h))
    par_blk = lambda r, off: pl.BlockSpec((r, M_DK), lambda i, h, off=off: (0, off + h))
    return pl.pallas_call(
        _mlstm_sample_kernel,
        out_shape=(
            jax.ShapeDtypeStruct((bsz, M_WIDTH), f32),
            jax.ShapeDtypeStruct((bsz, M_HEADS, M_DK, M_DV), f32),
            jax.ShapeDtypeStruct((bsz, M_HEADS * M_DK), f32),
            jax.ShapeDtypeStruct((M_HEADS, bsz, LANES), f32),
        ),
        grid=(bsz // g, M_HEADS),
        in_specs=[
            row_blk(M_DK, 0), row_blk(M_DK, M_HEADS), row_blk(M_DV, nvm), row_blk(M_DV, nvm + M_HEADS),
            pl.BlockSpec((g, LANES), lambda i, h: (i, 0)),
            pl.BlockSpec((CONV_W - 1, g, M_DK), lambda i, h: (0, i, h)),
            pl.BlockSpec((CONV_W - 1, g, M_DK), lambda i, h: (0, i, M_HEADS + h)),
            par_blk(CONV_W, 0), par_blk(CONV_W, M_HEADS), par_blk(1, 0), par_blk(1, M_HEADS),
            pl.BlockSpec((1, LANES), lambda i, h: (0, 0)),
            pl.BlockSpec((1, M_DV), lambda i, h: (0, h)),
            pl.BlockSpec((g, 1, M_DK, M_DV), lambda i, h: (i, h, 0, 0)),
            pl.BlockSpec((g, M_DK), lambda i, h: (i, h)),
            pl.BlockSpec((g, M_HEADS), lambda i, h: (i, 0)),
        ],
        out_specs=(
            pl.BlockSpec((g, M_DV), lambda i, h: (i, h)),
            pl.BlockSpec((g, 1, M_DK, M_DV), lambda i, h: (i, h, 0, 0)),
            pl.BlockSpec((g, M_DK), lambda i, h: (i, h)),
            pl.BlockSpec((1, g, LANES), lambda i, h: (h, i, 0)),
        ),
        scratch_shapes=[pltpu.VMEM((g, M_DV), f32)],
        compiler_params=_cparams(("parallel", "arbitrary")),
        name="mlstm_sample",
    )(proj, proj, proj, proj, ifp, state_conv, state_conv, conv_w, conv_w, conv_b, conv_b,
      bg_row, norm_w, c0, n0, m0)


def _moba_prompt_kernel(q_ref, k_ref, v_ref, qw_ref, kw_ref, slope_ref, o_ref, kn_ref, kb_scr, vb_scr, km_scr, *, nb):
    i = pl.program_id(2)
    blk = MOBA_BLOCK
    scale = B_DH ** -0.5

    @pl.when(i == 0)
    def _():
        kn = _rms(k_ref[...], kw_ref[0])
        kn_ref[...] = kn
        kb_scr[...] = kn.astype(bf16)
        vb_scr[...] = v_ref[...].astype(bf16)
        km_scr[...] = jnp.zeros_like(km_scr)
        km_scr[0:nb, :] = jnp.mean(kn.reshape(nb, blk, B_DH), axis=1)

    qn = _rms(q_ref[...], qw_ref[0])
    qb = qn.astype(bf16)

    km = km_scr[...]
    q0, q1, q2 = _split3(qn)
    k0, k1, k2 = _split3(km)
    gate = (_dot_nt(q0, k0) + _dot_nt(q0, k1) + _dot_nt(q1, k0)
            + _dot_nt(q1, k1) + _dot_nt(q0, k2) + _dot_nt(q2, k0))
    col = lax.broadcasted_iota(jnp.int32, gate.shape, 1)
    g = jnp.where(col < i, gate, NEG_INF)
    cnt = jnp.zeros(gate.shape, f32)
    for j in range(nb):
        gj = g[:, j:j + 1]
        beats = (gj > g) | ((gj == g) & (j < col))
        cnt = cnt + jnp.where(beats, 1.0, 0.0)
    sel = jnp.where((cnt < MOBA_TOPK) & (col < i), 1.0, 0.0)

    slope = slope_ref[0][:, 0:1]
    rr = lax.broadcasted_iota(jnp.int32, (blk, blk), 0)
    cc = lax.broadcasted_iota(jnp.int32, (blk, blk), 1)
    rel = (rr - cc).astype(f32)

    start = pl.multiple_of(i * blk, blk)
    s = _dot_nt(qb, kb_scr[pl.ds(start, blk), :]) * scale - slope * rel
    s = jnp.where(rel >= 0, s, NEG_INF)
    m0 = jnp.max(s, axis=-1, keepdims=True)
    p = jnp.exp(s - m0)
    l0 = jnp.sum(p, axis=-1, keepdims=True)
    acc0 = _dot(p.astype(bf16), vb_scr[pl.ds(start, blk), :])

    def body(j, carry):
        m, l, acc = carry
        st = pl.multiple_of(j * blk, blk)
        selj = jnp.sum(jnp.where(col == j, sel, 0.0), axis=-1, keepdims=True)
        dist = rel + ((i - j) * blk).astype(f32)
        sj = _dot_nt(qb, kb_scr[pl.ds(st, blk), :]) * scale - slope * dist
        sj = jnp.where(selj > 0, sj, NEG_INF)
        m_new = jnp.maximum(m, jnp.max(sj, axis=-1, keepdims=True))
        alpha = jnp.exp(m - m_new)
        pj = jnp.exp(sj - m_new)
        l = alpha * l + jnp.sum(pj, axis=-1, keepdims=True)
        acc = alpha * acc + _dot(pj.astype(bf16), vb_scr[pl.ds(st, blk), :])
        return m_new, l, acc

    m, l, acc = lax.fori_loop(0, i, body, (m0, l0, acc0))
    o_ref[...] = (acc / l).astype(o_ref.dtype)


def _moba_prompt(proj, qnorm_w, knorm_w, slopes, bsz, t):
    blk = MOBA_BLOCK
    nb = t // blk
    n = bsz * t
    cq = (C_VM + 2 * M_WIDTH) // B_DH
    ck = cq + B_HEADS
    cv = ck + B_HEADS
    return pl.pallas_call(
        functools.partial(_moba_prompt_kernel, nb=nb),
        out_shape=(jax.ShapeDtypeStruct((n, B_WIDTH), bf16), jax.ShapeDtypeStruct((n, B_WIDTH), f32)),
        grid=(bsz, B_HEADS, nb),
        in_specs=[
            pl.BlockSpec((blk, B_DH), lambda b, h, i: (b * nb + i, cq + h)),
            pl.BlockSpec((t, B_DH), lambda b, h, i: (b, ck + h)),
            pl.BlockSpec((t, B_DH), lambda b, h, i: (b, cv + h)),
            pl.BlockSpec((1, 1, B_DH), lambda b, h, i: (h, 0, 0)),
            pl.BlockSpec((1, 1, B_DH), lambda b, h, i: (h, 0, 0)),
            pl.BlockSpec((1, 1, LANES), lambda b, h, i: (h, 0, 0)),
        ],
        out_specs=(
            pl.BlockSpec((blk, B_DH), lambda b, h, i: (b * nb + i, h)),
            pl.BlockSpec((t, B_DH), lambda b, h, i: (b, h)),
        ),
        scratch_shapes=[pltpu.VMEM((t, B_DH), bf16), pltpu.VMEM((t, B_DH), bf16), pltpu.VMEM((LANES, B_DH), f32)],
        compiler_params=_cparams(("parallel", "parallel", "arbitrary")),
        name="moba_prompt",
    )(proj, proj, proj, qnorm_w, knorm_w, slopes)


def _moba_sample_kernel(pt_ref, q_ref, k_ref, v_ref, kc_ref, vc_ref, qw_ref, kw_ref, slope_ref,
                        o_ref, kn_ref, k_scr, v_scr, km_scr, *, past, page):
    del pt_ref
    p = pl.program_id(1)
    n_pages = pl.num_programs(1)
    blk = MOBA_BLOCK
    nb = past // blk
    tail = 2 * SUBLANES
    scale = B_DH ** -0.5

    kp = kc_ref[0]
    st = pl.multiple_of(p * page, page)
    k_scr[pl.ds(st, page), :] = kp.astype(bf16)
    v_scr[pl.ds(st, page), :] = vc_ref[0].astype(bf16)

    @pl.when(p == 0)
    def _():
        km_scr[...] = jnp.zeros_like(km_scr)

    jb = p // (blk // page)
    km_scr[pl.ds(jb, 1), :] = km_scr[pl.ds(jb, 1), :] + jnp.sum(kp, axis=0, keepdims=True)

    @pl.when(p == n_pages - 1)
    def _():
        qn = _rms(q_ref[0], qw_ref[...])
        kn = _rms(k_ref[0], kw_ref[...])
        kn_ref[0] = kn
        vn = v_ref[0]

        def to_row(a):
            return jnp.concatenate([a[h:h + 1, :] for h in range(B_HEADS)], axis=1)

        q_row = to_row(qn)
        zeros_tail = jnp.zeros((tail - 1, B_WIDTH), f32)
        k_scr[past:past + tail, :] = jnp.concatenate([to_row(kn), zeros_tail], axis=0).astype(bf16)
        v_scr[past:past + tail, :] = jnp.concatenate([to_row(vn), zeros_tail], axis=0).astype(bf16)

        prod = (km_scr[...] * (1.0 / blk)) * q_row
        lane = lax.broadcasted_iota(jnp.int32, (nb, LANES), 1)
        gate = jnp.zeros((nb, LANES), f32)
        for h in range(B_HEADS):
            gh = jnp.sum(prod[:, h * B_DH:(h + 1) * B_DH], axis=-1, keepdims=True)
            gate = gate + jnp.where(lane == h, gh, 0.0)
        row = lax.broadcasted_iota(jnp.int32, (nb, LANES), 0)
        cnt = jnp.zeros((nb, LANES), f32)
        for j in range(nb):
            gj = gate[j:j + 1, :]
            beats = (gj > gate) | ((gj == gate) & (j < row))
            cnt = cnt + jnp.where(beats, 1.0, 0.0)
        sel = jnp.where(cnt < MOBA_TOPK, 1.0, 0.0)

        qt = jnp.concatenate([qn] * B_HEADS, axis=1)
        hrow = lax.broadcasted_iota(jnp.int32, qt.shape, 0)
        hgrp = lax.broadcasted_iota(jnp.int32, qt.shape, 1) // B_DH
        qbd = jnp.where(hrow == hgrp, qt, 0.0)
        qbd = jnp.concatenate([qbd, jnp.zeros((LANES - B_HEADS, B_WIDTH), f32)], axis=0).astype(bf16)
        s = _dot_nt(k_scr[...], qbd)
        rows_all = lax.broadcasted_iota(jnp.int32, s.shape, 0)
        dist = (past - rows_all).astype(f32)
        tail_rows = lax.broadcasted_iota(jnp.int32, (tail, LANES), 0)
        mask = jnp.concatenate([jnp.broadcast_to(sel[j:j + 1, :], (blk, LANES)) for j in range(nb)]
                               + [jnp.where(tail_rows == 0, 1.0, 0.0)], axis=0)
        s = s * scale - slope_ref[...] * dist
        s = jnp.where(mask > 0, s, NEG_INF)
        mx = jnp.max(s, axis=0, keepdims=True)
        pr = jnp.exp(s - mx)
        pr = pr / jnp.sum(pr, axis=0, keepdims=True)
        o_all = _dot_tn(pr.astype(bf16), v_scr[...])
        orow = lax.broadcasted_iota(jnp.int32, (B_HEADS, B_DH), 0)
        out = jnp.zeros((B_HEADS, B_DH), f32)
        for h in range(B_HEADS):
            out = out + jnp.where(orow == h, o_all[0:B_HEADS, h * B_DH:(h + 1) * B_DH], 0.0)
        o_ref[0] = out


def _moba_sample(q, k, v, cache_k, cache_v, page_table, qnorm_w, knorm_w, slope_row):
    bsz, n_pages = page_table.shape
    page = cache_k.shape[1]
    past = n_pages * page
    tail = 2 * SUBLANES
    tok = pl.BlockSpec((1, B_HEADS, B_DH), lambda b, p, pt: (b, 0, 0))
    par = pl.BlockSpec((B_HEADS, B_DH), lambda b, p, pt: (0, 0))
    cache = pl.BlockSpec((1, page, B_WIDTH), lambda b, p, pt: (pt[b * n_pages + p], 0, 0))
    return pl.pallas_call(
        functools.partial(_moba_sample_kernel, past=past, page=page),
        out_shape=(jax.ShapeDtypeStruct((bsz, B_HEADS, B_DH), f32), jax.ShapeDtypeStruct((bsz, B_HEADS, B_DH), f32)),
        grid_spec=pltpu.PrefetchScalarGridSpec(
            num_scalar_prefetch=1,
            grid=(bsz, n_pages),
            in_specs=[tok, tok, tok, cache, cache, par, par, pl.BlockSpec((1, LANES), lambda b, p, pt: (0, 0))],
            out_specs=(tok, tok),
            scratch_shapes=[pltpu.VMEM((past + tail, B_WIDTH), bf16), pltpu.VMEM((past + tail, B_WIDTH), bf16),
                            pltpu.VMEM((past // MOBA_BLOCK, B_WIDTH), f32)],
        ),
        compiler_params=_cparams(("parallel", "arbitrary")),
        name="moba_sample",
    )(page_table.reshape(-1), q, k, v, cache_k, cache_v, qnorm_w, knorm_w, slope_row)


def _merge_kernel(x_ref, hg_ref, ob_ref, ga_ref, gb_ref, wa_ref, wb_ref, wo_ref, n2_ref, y_ref, hn_ref):
    ua = _dot(hg_ref[...].astype(bf16), wa_ref[...])
    ub = _dot(ob_ref[...].astype(bf16), wb_ref[...])
    z = jax.nn.sigmoid(ga_ref[...]) * ua + jax.nn.sigmoid(gb_ref[...]) * ub
    y = x_ref[...] + _dot(z.astype(bf16), wo_ref[...])
    y_ref[...] = y
    hn_ref[...] = _rms(y, n2_ref[...]).astype(bf16)


def _merge(x, hg, ob, proj, wa, wb, wo, norm2_w, tm):
    n, d = x.shape
    gblk = C_GATE // d
    return pl.pallas_call(
        _merge_kernel,
        out_shape=(jax.ShapeDtypeStruct((n, d), f32), jax.ShapeDtypeStruct((n, d), bf16)),
        grid=(n // tm,),
        in_specs=[
            pl.BlockSpec((tm, d), lambda i: (i, 0)),
            pl.BlockSpec((tm, M_WIDTH), lambda i: (i, 0)),
            pl.BlockSpec((tm, B_WIDTH), lambda i: (i, 0)),
            pl.BlockSpec((tm, d), lambda i: (i, gblk)),
            pl.BlockSpec((tm, d), lambda i: (i, gblk + 1)),
            _single((M_WIDTH, d), lambda i: (0, 0)),
            _single((B_WIDTH, d), lambda i: (0, 0)),
            _single((d, d), lambda i: (0, 0)),
            pl.BlockSpec((1, d), lambda i: (0, 0)),
        ],
        out_specs=(pl.BlockSpec((tm, d), lambda i: (i, 0)), pl.BlockSpec((tm, d), lambda i: (i, 0))),
        compiler_params=_cparams(("parallel",)),
        name="merge",
    )(x, hg, ob, proj, proj, wa, wb, wo, norm2_w)


def _top_values(s, k, out_scr):
    n_rows = s.shape[0]
    rows = lax.broadcasted_iota(jnp.int32, s.shape, 0)
    for r in range(k):
        mx = jnp.max(s, axis=0, keepdims=True)
        out_scr[r:r + 1, :] = mx
        if r + 1 < k:
            first = jnp.min(jnp.where(s == mx, rows, n_rows), axis=0, keepdims=True)
            s = jnp.where(rows == first, NEG_INF, s)


_PAIR_COUNTS = [P_TOPK // (a + 1) for a in range(P_TOPK)]


def _peer_select_kernel(hn_ref, wq_ref, k1_ref, k2_ref, s1_ref, s2_ref, a1_ref, a2_ref, tau_ref, v1_scr, v2_scr, top_scr):
    half = P_DK // 2
    qp = _dot(hn_ref[...], wq_ref[...])
    s1 = _dot_nt(k1_ref[0].astype(bf16), qp[:, :half].astype(bf16))
    s2 = _dot_nt(k2_ref[0].astype(bf16), qp[:, half:].astype(bf16))
    _top_values(s1, P_TOPK, v1_scr)
    _top_values(s2, P_TOPK, v2_scr)
    v1 = v1_scr[...]
    v2 = v2_scr[...]
    pieces = [v1[0:1, :] + v2]
    rows8 = lax.broadcasted_iota(jnp.int32, (SUBLANES, v1.shape[1]), 0)
    for a in range(1, SUBLANES):
        pieces.append(jnp.where(rows8 < _PAIR_COUNTS[a], v1[a:a + 1, :] + v2[0:SUBLANES, :], NEG_INF))
    pieces.append(v1[SUBLANES:P_TOPK, :] + v2[0:1, :])
    cand = jnp.concatenate(pieces, axis=0)
    _top_values(cand, P_TOPK, top_scr)
    top = top_scr[...]
    best = top[0:1, :]
    z = jnp.sum(jnp.exp(top - best), axis=0, keepdims=True)
    s1_ref[0] = s1
    s2_ref[0] = s2
    a1_ref[0] = jnp.exp(s1 - v1[0:1, :])
    a2_ref[0] = jnp.exp(s2 - v2[0:1, :]) / z
    tau_ref[0] = top[P_TOPK - 1:P_TOPK, :]


def _peer_select(hn, wq, keys1, keys2, tm):
    n, d = hn.shape
    half = P_DK // 2
    big = pl.BlockSpec((1, P_NKEYS, tm), lambda i, h: (h, 0, i))
    shape = jax.ShapeDtypeStruct((P_HEADS, P_NKEYS, n), f32)
    return pl.pallas_call(
        _peer_select_kernel,
        out_shape=(shape, shape, shape, shape, jax.ShapeDtypeStruct((P_HEADS, 1, n), f32)),
        grid=(n // tm, P_HEADS),
        in_specs=[
            pl.BlockSpec((tm, d), lambda i, h: (i, 0)),
            pl.BlockSpec((d, P_DK), lambda i, h: (0, h)),
            pl.BlockSpec((1, P_NKEYS, half), lambda i, h: (h, 0, 0)),
            pl.BlockSpec((1, P_NKEYS, half), lambda i, h: (h, 0, 0)),
        ],
        out_specs=(big, big, big, big, pl.BlockSpec((1, 1, tm), lambda i, h: (h, 0, i))),
        scratch_shapes=[pltpu.VMEM((P_TOPK, tm), f32)] * 3,
        compiler_params=_cparams(("parallel", "arbitrary")),
        name="peer_select",
    )(hn, wq, keys1, keys2)


def _peer_dense_kernel(hn_ref, u_ref, vt_ref, s1_ref, s2_ref, a1_ref, a2_ref, tau_ref, x_ref, y_ref,
                       act_scr, z_scr, acc_scr, *, lane_tile):
    c = pl.program_id(1)
    mc, tm = act_scr.shape
    groups = mc // P_NKEYS

    @pl.when(c == 0)
    def _():
        acc_scr[...] = jnp.zeros_like(acc_scr)

    act_scr[...] = _dot_nt(u_ref[...], hn_ref[...])

    def group(gi, carry):
        i1 = c * groups + gi
        r0 = pl.multiple_of(gi * P_NKEYS, P_NKEYS)
        for lt in range(tm // lane_tile):
            ls = slice(lt * lane_tile, (lt + 1) * lane_tile)
            a = act_scr[pl.ds(r0, P_NKEYS), ls]
            gelu = 0.5 * a * (1.0 + lax.erf(a * (2.0 ** -0.5)))
            w = jnp.zeros_like(a)
            for h in range(P_HEADS):
                s1_row = s1_ref[h, pl.ds(i1, 1), ls]
                a1_row = a1_ref[h, pl.ds(i1, 1), ls]
                hit = (s1_row + s2_ref[h, :, ls]) >= tau_ref[h, :, ls]
                w = w + jnp.where(hit, a2_ref[h, :, ls] * a1_row, 0.0)
            z_scr[pl.ds(r0, P_NKEYS), ls] = (w * gelu).astype(bf16)
        return carry

    lax.fori_loop(0, groups, group, 0)
    acc_scr[...] += _dot(vt_ref[...], z_scr[...])

    @pl.when(c == pl.num_programs(1) - 1)
    def _():
        y_ref[...] = x_ref[...] + acc_scr[...].T


def _peer_dense(hn, u, vt, s1, s2, a1, a2, tau, x, tm, mc):
    n, d = x.shape
    n_exp = u.shape[0]
    lane_tile = min(tm, 2 * LANES)
    sel = _single((P_HEADS, P_NKEYS, tm), lambda i, c: (0, 0, i))
    return pl.pallas_call(
        functools.partial(_peer_dense_kernel, lane_tile=lane_tile),
        out_shape=jax.ShapeDtypeStruct((n, d), f32),
        grid=(n // tm, n_exp // mc),
        in_specs=[
            pl.BlockSpec((tm, d), lambda i, c: (i, 0)),
            pl.BlockSpec((mc, d), lambda i, c: (c, 0)),
            pl.BlockSpec((d, mc), lambda i, c: (0, c)),
            sel, sel, sel, sel,
            _single((P_HEADS, 1, tm), lambda i, c: (0, 0, i)),
            _single((tm, d), lambda i, c: (i, 0)),
        ],
        out_specs=pl.BlockSpec((tm, d), lambda i, c: (i, 0)),
        scratch_shapes=[pltpu.VMEM((mc, tm), f32), pltpu.VMEM((mc, tm), bf16), pltpu.VMEM((d, tm), f32)],
        compiler_params=_cparams(("parallel", "arbitrary")),
        name="peer_dense",
    )(hn, u, vt, s1, s2, a1, a2, tau, x)


def _pick_tile(n, pref):
    t = min(n, pref)
    while n % t:
        t //= 2
    return t


def _prep_weights(p):
    d = p["w_in"].shape[0]
    w = p["w_in"]
    o_if = 2 * M_HEADS * M_DK + 2 * M_WIDTH
    o_qb = o_if + 2 * M_HEADS
    o_gate = o_qb + 3 * B_WIDTH
    w_main = jnp.concatenate([w[:, :2 * M_HEADS * M_DK], w[:, o_gate:], w[:, 2 * M_HEADS * M_DK:o_if], w[:, o_qb:o_gate]],
                             axis=1).astype(bf16)
    w_if = jnp.pad(w[:, o_if:o_qb], ((0, 0), (0, LANES - 2 * M_HEADS))).astype(bf16)
    bg = p["b_gates"].astype(f32)
    heads = jnp.arange(1, B_HEADS + 1, dtype=f32)
    slopes = 2.0 ** (-8.0 * heads / B_HEADS)
    return dict(
        w_main=w_main, w_if=w_if,
        norm1=p["norm1_w"].reshape(1, d), norm2=p["norm2_w"].reshape(1, d),
        conv_w=p["conv_w"], conv_b=p["conv_b"].reshape(1, -1),
        bg_row=jnp.pad(bg, (0, LANES - bg.shape[0])).reshape(1, LANES), bg_col=bg.reshape(-1, 1),
        mnorm=p["mlstm_norm_w"].reshape(1, M_WIDTH),
        qnorm=p["qnorm_w"], knorm=p["knorm_w"],
        slopes_head=jnp.broadcast_to(slopes[:, None, None], (B_HEADS, 1, LANES)),
        slope_row=jnp.pad(slopes, (0, LANES - B_HEADS)).reshape(1, LANES),
        wa=p["w_branch_a"].astype(bf16), wb=p["w_branch_b"].astype(bf16), wo=p["w_out"].astype(bf16),
        wq=p["peer_wq"].astype(bf16), keys1=p["peer_keys1"], keys2=p["peer_keys2"],
        u=p["peer_u"].astype(bf16), vt=p["peer_v"].T.astype(bf16),
    )


def _channel_mixer(x1, hn2, w):
    n = x1.shape[0]
    s1, s2, a1, a2, tau = _peer_select(hn2, w["wq"], w["keys1"], w["keys2"], _pick_tile(n, 256))
    return _peer_dense(hn2, w["u"], w["vt"], s1, s2, a1, a2, tau, x1, _pick_tile(n, 512), 1024)


def _layer_prompt(x, w):
    bsz, t, d = x.shape
    n = bsz * t
    x2 = x.reshape(n, d)
    proj, ifp = _inproj(x2, w["norm1"], w["w_main"], w["w_if"], _pick_tile(n, 512), 1024)
    ifp_t = ifp[:, :2 * M_HEADS].T
    hg, c1, n1, m1 = _mlstm_prompt(proj, ifp, ifp_t, w["conv_w"], w["conv_b"], w["bg_row"], w["bg_col"], w["mnorm"], bsz, t)
    ob, kn = _moba_prompt(proj, w["qnorm"].reshape(B_HEADS, 1, B_DH), w["knorm"].reshape(B_HEADS, 1, B_DH),
                          w["slopes_head"], bsz, t)
    x1, hn2 = _merge(x2, hg, ob, proj, w["wa"], w["wb"], w["wo"], w["norm2"], _pick_tile(n, 256))
    y = _channel_mixer(x1, hn2, w)
    qkw = 2 * M_HEADS * M_DK
    new_conv = proj[:, :qkw].reshape(bsz, t, qkw)[:, t - (CONV_W - 1):]
    c_vb = C_VM + 2 * M_WIDTH + 2 * B_WIDTH
    v_b = proj[:, c_vb:c_vb + B_WIDTH].reshape(bsz, t, B_HEADS, B_DH)
    return y.reshape(bsz, t, d), (kn.reshape(bsz, t, B_HEADS, B_DH), v_b, c1, n1, m1[:, :M_HEADS, 0], new_conv)


def _layer_sample(x, conv_buf, c0, n0, m0, cache_k, cache_v, page_table, w):
    bsz, t, d = x.shape
    x2 = x.reshape(bsz, d)
    proj, ifp = _inproj(x2, w["norm1"], w["w_main"], w["w_if"], _pick_tile(bsz, 512), 1024)
    hg, c1, n1, m1 = _mlstm_sample(proj, ifp, jnp.swapaxes(conv_buf, 0, 1), w["conv_w"], w["conv_b"], w["bg_row"], w["mnorm"],
                                   c0, n0.reshape(bsz, M_HEADS * M_DK), m0)
    c_qb = C_VM + 2 * M_WIDTH
    heads = lambda off: proj[:, off:off + B_WIDTH].reshape(bsz, B_HEADS, B_DH)
    n_pool, page = cache_k.shape[0], cache_k.shape[1]
    ob, kn = _moba_sample(heads(c_qb), heads(c_qb + B_WIDTH), heads(c_qb + 2 * B_WIDTH),
                          cache_k.reshape(n_pool, page, B_WIDTH), cache_v.reshape(n_pool, page, B_WIDTH),
                          page_table, w["qnorm"], w["knorm"], w["slope_row"])
    x1, hn2 = _merge(x2, hg, ob.reshape(bsz, B_WIDTH), proj, w["wa"], w["wb"], w["wo"], w["norm2"], _pick_tile(bsz, 256))
    y = _channel_mixer(x1, hn2, w)
    qkw = 2 * M_HEADS * M_DK
    new_conv = jnp.concatenate([conv_buf[:, 1:], proj[:, None, :qkw]], axis=1)
    v_b = heads(c_qb + 2 * B_WIDTH).reshape(bsz, 1, B_HEADS, B_DH)
    return y.reshape(bsz, t, d), (kn.reshape(bsz, 1, B_HEADS, B_DH), v_b, c1, n1.reshape(bsz, M_HEADS, M_DK),
                                  m1[:, :, 0].T, new_conv)


def kernel(x_prompt, x_sample, cache_k, cache_v, state_C, state_n, state_m, state_conv, page_table, norm1_w, w_in, b_gates, conv_w, conv_b, mlstm_norm_w, qnorm_w, knorm_w, w_branch_a, w_branch_b, w_out, norm2_w, peer_wq, peer_keys1, peer_keys2, peer_u, peer_v):
    depth = w_in.shape[0]
    assert x_sample.shape[1] == 1, "the sample group decodes one token per sequence"
    x_p, x_s = x_prompt, x_sample
    states_p, states_s = [], []
    for l in range(depth):
        w = _prep_weights(dict(
            norm1_w=norm1_w[l], w_in=w_in[l], b_gates=b_gates[l], conv_w=conv_w[l], conv_b=conv_b[l],
            mlstm_norm_w=mlstm_norm_w[l], qnorm_w=qnorm_w[l], knorm_w=knorm_w[l], w_branch_a=w_branch_a[l],
            w_branch_b=w_branch_b[l], w_out=w_out[l], norm2_w=norm2_w[l], peer_wq=peer_wq[l],
            peer_keys1=peer_keys1[l], peer_keys2=peer_keys2[l], peer_u=peer_u[l], peer_v=peer_v[l]))
        x_p, st_p = _layer_prompt(x_p, w)
        x_s, st_s = _layer_sample(x_s, state_conv[l], state_C[l], state_n[l], state_m[l],
                                  cache_k[l], cache_v[l], page_table, w)
        states_p.append(st_p)
        states_s.append(st_s)
    k_p, v_p, c_p, n_p, m_p, conv_p = [jnp.stack(a) for a in zip(*states_p)]
    k_s, v_s, c_s, n_s, m_s, conv_s = [jnp.stack(a) for a in zip(*states_s)]
    return (x_p, x_s, k_p, v_p, c_p, n_p, m_p, conv_p, k_s, v_s, c_s, n_s, m_s, conv_s)
```
